```python
import jax, jax.numpy as jnp
from jax import lax
import numpy as np

D_MODEL = 4096
BATCH = 4
SEQ = 2048
DEPTH = 4
DEC_BATCH = 128
DEC_SEQ = 1
PAST_LEN = 16384
PAGE_SIZE = 128

H_GLA = 4
DK_GLA = D_MODEL // 2
DV_GLA = D_MODEL
DK_HEAD = DK_GLA // H_GLA
DV_HEAD = DV_GLA // H_GLA
GATE_RANK = 16
GATE_TAU = 16.0
GLA_CHUNK = 64
G_SG = 4
D_SG = D_MODEL
DG_SG = D_SG // G_SG
SG_CHUNK = 128
D_FF = ((8 * D_MODEL // 3 + 255) // 256) * 256
CONV_W = 3
N_MOD = 6
EPS = 1e-6
SPLIT_SIZES = (DK_GLA, DK_GLA, DV_GLA, DV_GLA, GATE_RANK, D_SG, D_SG, D_MODEL, D_MODEL)
N_IN = DK_GLA * 2 + DV_GLA * 2 + GATE_RANK + D_SG * 2 + D_MODEL * 2

kernel_name = "gla_gmlp_convffn_adaln_hybrid_step"


def rmsnorm(x, g):
    xf = x.astype(jnp.float32)
    y = xf * lax.rsqrt(jnp.mean(xf * xf, axis=-1, keepdims=True) + EPS)
    return (y * g.astype(jnp.float32)).astype(x.dtype)


def layernorm(x, g, b):
    xf = x.astype(jnp.float32)
    mu = jnp.mean(xf, axis=-1, keepdims=True)
    xc = xf - mu
    y = xc * lax.rsqrt(jnp.mean(xc * xc, axis=-1, keepdims=True) + EPS)
    return (y * g.astype(jnp.float32) + b.astype(jnp.float32)).astype(x.dtype)


def split_columns(z):
    outs, off = [], 0
    for s in SPLIT_SIZES:
        outs.append(z[..., off:off + s])
        off += s
    return outs


def gla_chunked(q, k, v, log_a, s0):
    B, T = q.shape[0], q.shape[1]
    C = min(GLA_CHUNK, T)
    pad = (-T) % C
    f32 = jnp.float32
    q, k, v, log_a = (a.astype(f32) for a in (q, k, v, log_a))
    if pad:
        pw = ((0, 0), (0, pad), (0, 0), (0, 0))
        q, k, v, log_a = (jnp.pad(a, pw) for a in (q, k, v, log_a))
    N = (T + pad) // C

    def to_chunks(a):
        return a.reshape((B, N, C) + a.shape[2:]).swapaxes(0, 1)

    causal = jnp.tril(jnp.ones((C, C), dtype=bool))[None, :, :, None, None]

    def step(S, xs):
        qc, kc, vc, lac = xs
        b = jnp.cumsum(lac, axis=1)
        o_inter = jnp.einsum('bchk,bhkv->bchv', qc * jnp.exp(b), S)
        diff = b[:, :, None] - b[:, None, :]
        decay = jnp.exp(jnp.where(causal, diff, -jnp.inf))
        attn = jnp.einsum('bihk,bijhk,bjhk->bhij', qc, decay, kc)
        o_intra = jnp.einsum('bhij,bjhv->bihv', attn, vc)
        b_last = b[:, -1]
        k_dec = kc * jnp.exp(b_last[:, None] - b)
        S_new = jnp.exp(b_last)[..., None] * S + jnp.einsum('bjhk,bjhv->bhkv', k_dec, vc)
        return S_new, o_inter + o_intra

    S_fin, o = lax.scan(step, s0.astype(f32), (to_chunks(q), to_chunks(k), to_chunks(v), to_chunks(log_a)))
    o = o.swapaxes(0, 1).reshape(B, N * C, H_GLA, DV_HEAD)[:, :T]
    return S_fin, o


def spatial_gate(u, v, w_s, b_s):
    B, T = u.shape[0], u.shape[1]
    L = min(SG_CHUNK, T)
    N = T // L
    w = jnp.tril(w_s[:, :L, :L])
    vv = v.reshape(B, N, L, G_SG, DG_SG)
    mixed = jnp.einsum('gts,bnsgd->bntgd', w, vv) + b_s[:, :L].T[None, None, :, :, None]
    return u * mixed.reshape(B, T, D_SG).astype(u.dtype)


def run_trunk(x, c, s_gla, conv_buf, w_ada, b_ada, norm1_g, w_in, w_gate2, b_gate2, gla_norm_g,
              sg_ln_g, sg_ln_b, w_s, b_s, w_o, norm2_g, w_up, conv_w, conv_b, w_down, final_g):
    B, T = x.shape[0], x.shape[1]
    new_s, new_conv, v_rows = [], [], []
    for l in range(DEPTH):
        mod = jax.nn.silu(c) @ w_ada[l] + b_ada[l]
        sh1, sc1, g1, sh2, sc2, g2 = jnp.split(mod[:, None, :], N_MOD, axis=-1)
        h = rmsnorm(x, norm1_g[l]) * (1.0 + sc1) + sh1
        z = h @ w_in[l]
        q, k, vg, r, a_lr, u, vs, ga, gb = split_columns(z)
        log_a = jax.nn.log_sigmoid((a_lr @ w_gate2[l] + b_gate2[l]).astype(jnp.float32)) / GATE_TAU
        q = q.reshape(B, T, H_GLA, DK_HEAD) * (DK_HEAD ** -0.5)
        k = k.reshape(B, T, H_GLA, DK_HEAD)
        vg = vg.reshape(B, T, H_GLA, DV_HEAD)
        log_a = log_a.reshape(B, T, H_GLA, DK_HEAD)
        S_fin, o = gla_chunked(q, k, vg, log_a, s_gla[l])
        new_s.append(S_fin.astype(s_gla.dtype))
        o = rmsnorm(o.astype(x.dtype), gla_norm_g[l].reshape(H_GLA, DV_HEAD)).reshape(B, T, DV_GLA)
        o_a = o * jax.nn.silu(r)
        u = jax.nn.gelu(u)
        vs = layernorm(jax.nn.gelu(vs), sg_ln_g[l], sg_ln_b[l])
        v_rows.append(vs)
        o_b = spatial_gate(u, vs, w_s[l], b_s[l])
        merged = jax.nn.sigmoid(ga) * o_a + jax.nn.sigmoid(gb) * o_b
        x = x + g1 * (merged @ w_o[l])
        h2 = rmsnorm(x, norm2_g[l]) * (1.0 + sc2) + sh2
        up = h2 @ w_up[l]
        up_pad = jnp.concatenate([conv_buf[l].astype(up.dtype), up], axis=1)
        cw = conv_w[l]
        conv = conv_b[l] + cw[0] * up_pad[:, 0:T]
        for i in range(1, CONV_W):
            conv = conv + cw[i] * up_pad[:, i:i + T]
        new_conv.append(up_pad[:, -(CONV_W - 1):].astype(conv_buf.dtype))
        a_val, a_gate = jnp.split(conv, 2, axis=-1)
        x = x + g2 * ((jax.nn.silu(a_gate) * a_val) @ w_down[l])
    y = rmsnorm(x, final_g)
    return y, jnp.stack(new_s), jnp.stack(new_conv), v_rows


def setup_inputs(seed: int = 0) -> dict:
    key = jax.random.key(seed)
    ks = jax.random.split(key, 32)
    f32 = jnp.float32
    nrm = lambda k, shape, s: jax.random.normal(k, shape, f32) * s
    D, F2 = D_MODEL, 2 * D_FF
    return {
        "x_prompt": nrm(ks[0], (BATCH, SEQ, D), 1.0),
        "x_sample": nrm(ks[1], (DEC_BATCH, DEC_SEQ, D), 1.0),
        "state_gla": nrm(ks[2], (DEPTH, DEC_BATCH, H_GLA, DK_HEAD, DV_HEAD), 0.5),
        "state_conv": nrm(ks[3], (DEPTH, DEC_BATCH, CONV_W - 1, F2), 1.0),
        "c_prompt": nrm(ks[4], (BATCH, D), 1.0),
        "c_sample": nrm(ks[5], (DEC_BATCH, D), 1.0),
        "w_ada": nrm(ks[6], (DEPTH, D, N_MOD * D), 0.5 * D ** -0.5),
        "b_ada": nrm(ks[7], (DEPTH, N_MOD * D), 0.02),
        "norm1_g": 1.0 + nrm(ks[8], (DEPTH, D), 0.02),
        "w_in": nrm(ks[9], (DEPTH, D, N_IN), D ** -0.5),
        "w_gate2": nrm(ks[10], (DEPTH, GATE_RANK, DK_GLA), GATE_RANK ** -0.5),
        "b_gate2": nrm(ks[11], (DEPTH, DK_GLA), 0.02),
        "gla_norm_g": 1.0 + nrm(ks[12], (DEPTH, DV_GLA), 0.02),
        "sg_ln_g": 1.0 + nrm(ks[13], (DEPTH, D_SG), 0.02),
        "sg_ln_b": nrm(ks[14], (DEPTH, D_SG), 0.02),
        "w_s": nrm(ks[15], (DEPTH, G_SG, SG_CHUNK, SG_CHUNK), SG_CHUNK ** -0.5),
        "b_s": 1.0 + nrm(ks[16], (DEPTH, G_SG, SG_CHUNK), 0.02),
        "w_o": nrm(ks[17], (DEPTH, D, D), D ** -0.5),
        "norm2_g": 1.0 + nrm(ks[18], (DEPTH, D), 0.02),
        "w_up": nrm(ks[19], (DEPTH, D, F2), D ** -0.5),
        "conv_w": nrm(ks[20], (DEPTH, CONV_W, F2), CONV_W ** -0.5),
        "conv_b": nrm(ks[21], (DEPTH, F2), 0.02),
        "w_down": nrm(ks[22], (DEPTH, D_FF, D), D_FF ** -0.5),
        "final_g": 1.0 + nrm(ks[23], (D,), 0.02),
    }


def reference(x_prompt, x_sample, state_gla, state_conv, c_prompt, c_sample, w_ada, b_ada, norm1_g, w_in,
              w_gate2, b_gate2, gla_norm_g, sg_ln_g, sg_ln_b, w_s, b_s, w_o, norm2_g, w_up, conv_w, conv_b,
              w_down, final_g):
    B = x_prompt.shape[0]
    s0_prompt = jnp.zeros((DEPTH, B, H_GLA, DK_HEAD, DV_HEAD), x_prompt.dtype)
    conv0_prompt = jnp.zeros((DEPTH, B, CONV_W - 1, 2 * D_FF), x_prompt.dtype)
    y_prompt, new_gla_prompt, new_conv_prompt, _ = run_trunk(
        x_prompt, c_prompt, s0_prompt, conv0_prompt, w_ada, b_ada, norm1_g, w_in, w_gate2, b_gate2,
        gla_norm_g, sg_ln_g, sg_ln_b, w_s, b_s, w_o, norm2_g, w_up, conv_w, conv_b, w_down, final_g)
    y_sample, new_gla_sample, new_conv_sample, v_rows_sample = run_trunk(
        x_sample, c_sample, state_gla, state_conv, w_ada, b_ada, norm1_g, w_in, w_gate2, b_gate2,
        gla_norm_g, sg_ln_g, sg_ln_b, w_s, b_s, w_o, norm2_g, w_up, conv_w, conv_b, w_down, final_g)
    new_chunk_v_sample = jnp.stack(v_rows_sample)
    return (y_prompt, y_sample, new_gla_prompt, new_conv_prompt, new_gla_sample, new_conv_sample, new_chunk_v_sample)
```

```python
import functools
from typing import NamedTuple

import jax
import jax.numpy as jnp
from jax import lax
from jax.experimental import pallas as pl
from jax.experimental.pallas import tpu as pltpu

F32 = jnp.float32
BF16 = jnp.bfloat16

D = 4096
DEPTH = 4
N_HEADS = 4
DK = 512
DV = 1024
RANK = 16
TAU = 16.0
SG_CHUNK = 128
F = 11008
N_MOD = 6
EPS = 1e-6

QKVR_COLS = 2 * N_HEADS * DK + 2 * D
ALR_COL = QKVR_COLS
TAIL_COL = QKVR_COLS + RANK

LANES = 128
VMEM_CAP = 56 * 2 ** 20
EXP_CLAMP = 80.0


def _vmem(nbytes):
    return int(min(VMEM_CAP, nbytes + 16 * 2 ** 20))


def _params(semantics, nbytes):
    return pltpu.CompilerParams(dimension_semantics=semantics, vmem_limit_bytes=_vmem(nbytes))


def _silu(x):
    return x * jax.nn.sigmoid(x)


def _gelu_tanh(x):
    return 0.5 * x * (1.0 + jnp.tanh(0.7978845608028654 * (x + 0.044715 * x * x * x)))


def _log_sigmoid(x):
    return jnp.minimum(x, 0.0) - jnp.log(1.0 + jnp.exp(-jnp.abs(x)))


def _split_bf16(x):
    hi = x.astype(BF16)
    lo = (x - hi.astype(F32)).astype(BF16)
    return hi, lo


def _dot(a, b):
    return jnp.dot(a, b, preferred_element_type=F32)


def _dot_f32(a, b):
    a_hi, a_lo = _split_bf16(a)
    b_hi, b_lo = _split_bf16(b)
    return _dot(a_hi, b_hi) + _dot(a_hi, b_lo) + _dot(a_lo, b_hi)


class _Mod(NamedTuple):
    arr: jax.Array
    rows_per_group: int


def _mod_spec(mod, layer, comp, tm, tn):
    rows = mod.arr.shape[2]
    nblk = D // tn
    if rows == 1:
        return pl.BlockSpec((None, None, 1, tn),
                            lambda i, *r: (layer, (i * tm) // mod.rows_per_group, 0, comp * nblk + (r[0] if r else 0)))
    assert rows % tm == 0
    return pl.BlockSpec((None, None, tm, tn), lambda i, *r: (layer, 0, i, comp * nblk + (r[0] if r else 0)))


def _ada_kernel(c_ref, w_ref, b_ref, o_ref):
    c = c_ref[...]
    o_ref[...] = _dot(_silu(c).astype(BF16), w_ref[...].astype(BF16)) + b_ref[...]


def _ada(c_all, w_ada, b_ada):
    rows = c_all.shape[0]
    tn = 512
    n = N_MOD * D
    return pl.pallas_call(
        _ada_kernel,
        grid=(DEPTH, n // tn),
        in_specs=[pl.BlockSpec((rows, D), lambda l, j: (0, 0)),
                  pl.BlockSpec((None, D, tn), lambda l, j: (l, 0, j)),
                  pl.BlockSpec((None, 1, tn), lambda l, j: (l, 0, j))],
        out_specs=pl.BlockSpec((None, rows, tn), lambda l, j: (l, 0, j)),
        out_shape=jax.ShapeDtypeStruct((DEPTH, rows, n), F32),
        compiler_params=_params(("arbitrary", "arbitrary"), 2 * (rows * D * 4 + D * tn * 4) + D * tn * 2),
        name="ada_mod",
    )(c_all, w_ada, b_ada.reshape(DEPTH, 1, n))


def _norm_kernel(*refs, modulate):
    if modulate:
        x_ref, g_ref, sc_ref, sh_ref, o_ref = refs
    else:
        x_ref, g_ref, o_ref = refs
    x = x_ref[...]
    y = x * lax.rsqrt(jnp.mean(x * x, axis=-1, keepdims=True) + EPS) * g_ref[...]
    if modulate:
        y = y * (1.0 + sc_ref[...]) + sh_ref[...]
    o_ref[...] = y.astype(o_ref.dtype)


def _norm(x, g_row, tm, out_dtype, mod=None, layer=None, comp_shift=None, comp_scale=None):
    m = x.shape[0]
    in_specs = [pl.BlockSpec((tm, D), lambda i: (i, 0)), pl.BlockSpec((1, D), lambda i: (0, 0))]
    args = [x, g_row]
    if mod is not None:
        in_specs += [_mod_spec(mod, layer, comp_scale, tm, D), _mod_spec(mod, layer, comp_shift, tm, D)]
        args += [mod.arr, mod.arr]
    return pl.pallas_call(
        functools.partial(_norm_kernel, modulate=mod is not None),
        grid=(m // tm,),
        in_specs=in_specs,
        out_specs=pl.BlockSpec((tm, D), lambda i: (i, 0)),
        out_shape=jax.ShapeDtypeStruct((m, D), out_dtype),
        compiler_params=_params(("arbitrary",), 2 * tm * D * 4 * 4),
        name="norm_mod" if mod is not None else "norm",
    )(*args)


def _matmul_kernel(*refs, nk, residual):
    if residual:
        a_ref, w_ref, res_ref, gate_ref, o_ref = refs[:5]
    else:
        a_ref, w_ref, o_ref = refs[:3]
    acc_ref = refs[-1] if nk > 1 else None
    part = _dot(a_ref[...], w_ref[...].astype(BF16))

    def finish(acc):
        if residual:
            acc = res_ref[...] + gate_ref[...] * acc
        o_ref[...] = acc.astype(o_ref.dtype)

    if nk == 1:
        finish(part)
    else:
        kk = pl.program_id(2)

        @pl.when(kk == 0)
        def _():
            acc_ref[...] = part

        @pl.when(kk > 0)
        def _():
            acc_ref[...] += part

        @pl.when(kk == nk - 1)
        def _():
            finish(acc_ref[...])


def _matmul(a, w, layer, col_block, n, tm, tn, nk=1, out_dtype=F32, res=None, mod=None, gate_comp=None, name="proj"):
    m, k = a.shape
    tk = k // nk
    assert m % tm == 0 and n % tn == 0 and k % nk == 0
    residual = res is not None
    in_specs = [pl.BlockSpec((tm, tk), lambda i, j, kk: (i, kk)),
                pl.BlockSpec((None, tk, tn), lambda i, j, kk: (layer, kk, col_block + j))]
    args = [a, w]
    nbytes = 2 * (tm * tk * 2 + tk * tn * w.dtype.itemsize + tm * tn * 4) + tk * tn * 2 + 2 * tm * tn * 4
    if residual:
        in_specs += [pl.BlockSpec((tm, tn), lambda i, j, kk: (i, j)), _mod_spec(mod, layer, gate_comp, tm, tn)]
        args += [res, mod.arr]
        nbytes += 2 * 2 * tm * tn * 4
    return pl.pallas_call(
        functools.partial(_matmul_kernel, nk=nk, residual=residual),
        grid=(m // tm, n // tn, nk),
        in_specs=in_specs,
        out_specs=pl.BlockSpec((tm, tn), lambda i, j, kk: (i, j)),
        out_shape=jax.ShapeDtypeStruct((m, n), out_dtype),
        scratch_shapes=[pltpu.VMEM((tm, tn), F32)] if nk > 1 else [],
        compiler_params=_params(("arbitrary", "arbitrary", "arbitrary"), nbytes),
        name=name,
    )(*args)


def _gate_log_decay(alr, wg_ref, bg_ref):
    return _log_sigmoid(_dot_f32(alr, wg_ref[...]) + bg_ref[...]) * (1.0 / TAU)


def _gla_prefill_kernel(q_ref, k_ref, v_ref, alr_ref, wg_ref, bg_ref, *rest, chunk, sub):
    o_ref, s_ref = rest[-2:]

    @pl.when(pl.program_id(2) == 0)
    def _():
        s_ref[...] = jnp.zeros_like(s_ref)

    q = q_ref[...] * (DK ** -0.5)
    k = k_ref[...]
    v = v_ref[...].astype(BF16)
    la = _gate_log_decay(alr_ref[...], wg_ref, bg_ref)
    row = lax.broadcasted_iota(jnp.int32, (chunk, chunk), 0)
    col = lax.broadcasted_iota(jnp.int32, (chunk, chunk), 1)
    causal = col <= row
    tri = jnp.where(causal, 1.0, 0.0).astype(BF16)
    la_hi, la_lo = _split_bf16(la)
    b = _dot(tri, la_hi) + _dot(tri, la_lo)
    s_old = s_ref[...]
    o = _dot((q * jnp.exp(b)).astype(BF16), s_old.astype(BF16))
    blocks = []
    for r0 in range(0, chunk, sub):
        mid = b[r0 + sub // 2 - 1:r0 + sub // 2, :]
        qt = q[r0:r0 + sub] * jnp.exp(jnp.clip(b[r0:r0 + sub] - mid, -EXP_CLAMP, EXP_CLAMP))
        kt = k * jnp.exp(jnp.clip(mid - b, -EXP_CLAMP, EXP_CLAMP))
        blocks.append(lax.dot_general(qt.astype(BF16), kt.astype(BF16), (((1,), (1,)), ((), ())),
                                      preferred_element_type=F32))
    scores = jnp.where(causal, jnp.concatenate(blocks, axis=0), 0.0)
    o_ref[...] = o + _dot(scores.astype(BF16), v)
    k_t = k.T
    b_t = b.T
    b_last = b_t[:, chunk - 1:chunk]
    s_ref[...] = jnp.exp(b_last) * s_old + _dot((k_t * jnp.exp(b_last - b_t)).astype(BF16), v)


def _gla_prefill(z_a, alr, wg, bg, layer, state_buf, batch, seq):
    chunk, sub = 128, 32
    nc = seq // chunk
    in_specs = [pl.BlockSpec((chunk, DK), lambda b, h, n: (b * nc + n, h)),
                pl.BlockSpec((chunk, DK), lambda b, h, n: (b * nc + n, N_HEADS + h)),
                pl.BlockSpec((chunk, DV), lambda b, h, n: (b * nc + n, N_HEADS + h)),
                pl.BlockSpec((chunk, LANES), lambda b, h, n: (b * nc + n, 0)),
                pl.BlockSpec((None, LANES, DK), lambda b, h, n: (layer, 0, h)),
                pl.BlockSpec((None, 1, DK), lambda b, h, n: (layer, 0, h))]
    args = [z_a, z_a, z_a, alr, wg, bg]
    aliases = {}
    if state_buf is not None:
        in_specs.append(pl.BlockSpec(memory_space=pl.ANY))
        args.append(state_buf)
        aliases = {len(args) - 1: 1}
    return pl.pallas_call(
        functools.partial(_gla_prefill_kernel, chunk=chunk, sub=sub),
        grid=(batch, N_HEADS, nc),
        in_specs=in_specs,
        out_specs=[pl.BlockSpec((chunk, DV), lambda b, h, n: (b * nc + n, h)),
                   pl.BlockSpec((None, None, None, DK, DV), lambda b, h, n: (layer, b, h, 0, 0))],
        out_shape=[jax.ShapeDtypeStruct((batch * seq, N_HEADS * DV), F32),
                   jax.ShapeDtypeStruct((DEPTH, batch, N_HEADS, DK, DV), F32)],
        input_output_aliases=aliases,
        compiler_params=_params(("arbitrary", "arbitrary", "arbitrary"), 4 * DK * DV * 4),
        name="gla_prefill",
    )(*args)


def _gla_decode_kernel(q_ref, k_ref, v_ref, alr_ref, wg_ref, bg_ref, s_ref, *rest):
    o_ref, so_ref = rest[-2:]
    q = q_ref[...] * (DK ** -0.5)
    la = _gate_log_decay(jnp.broadcast_to(alr_ref[...], (8, LANES)), wg_ref, bg_ref)[0:1]
    rid = lax.broadcasted_iota(jnp.int32, (LANES, DK), 0)
    rows = jnp.where(rid == 0, q, jnp.where(rid == 1, k_ref[...], jnp.where(rid == 2, la, 0.0)))
    cols = rows.T
    s_new = jnp.exp(cols[:, 2:3]) * s_ref[...] + cols[:, 1:2] * v_ref[...]
    so_ref[...] = s_new
    o_ref[...] = jnp.sum(cols[:, 0:1] * s_new, axis=0, keepdims=True)


def _gla_decode(z_a, alr, wg, bg, layer, state_in, state_buf):
    batch = z_a.shape[0]
    z3 = z_a.reshape(batch, 1, QKVR_COLS)
    in_specs = [pl.BlockSpec((None, 1, DK), lambda b, h: (b, 0, h)),
                pl.BlockSpec((None, 1, DK), lambda b, h: (b, 0, N_HEADS + h)),
                pl.BlockSpec((None, 1, DV), lambda b, h: (b, 0, N_HEADS + h)),
                pl.BlockSpec((None, 1, LANES), lambda b, h: (b, 0, 0)),
                pl.BlockSpec((None, LANES, DK), lambda b, h: (layer, 0, h)),
                pl.BlockSpec((None, 1, DK), lambda b, h: (layer, 0, h)),
                pl.BlockSpec((None, None, None, DK, DV), lambda b, h: (layer, b, h, 0, 0))]
    args = [z3, z3, z3, alr.reshape(batch, 1, LANES), wg, bg, state_in]
    aliases = {}
    if state_buf is not None:
        in_specs.append(pl.BlockSpec(memory_space=pl.ANY))
        args.append(state_buf)
        aliases = {len(args) - 1: 1}
    o, state = pl.pallas_call(
        _gla_decode_kernel,
        grid=(batch, N_HEADS),
        in_specs=in_specs,
        out_specs=[pl.BlockSpec((None, 1, DV), lambda b, h: (b, 0, h)),
                   pl.BlockSpec((None, None, None, DK, DV), lambda b, h: (layer, b, h, 0, 0))],
        out_shape=[jax.ShapeDtypeStruct((batch, 1, N_HEADS * DV), F32),
                   jax.ShapeDtypeStruct((DEPTH, batch, N_HEADS, DK, DV), F32)],
        input_output_aliases=aliases,
        compiler_params=_params(("arbitrary", "arbitrary"), 4 * DK * DV * 4 + 3 * DK * DV * 4),
        name="gla_decode",
    )(*args)
    return o.reshape(batch, N_HEADS * DV), state


def _mixer_kernel(o_ref, r_ref, u_ref, vs_ref, ga_ref, gb_ref, gn_ref, lg_ref, lb_ref, sgw_ref, sgb_ref,
                  *outs, decode):
    m_ref = outs[0]
    vs = _gelu_tanh(vs_ref[...])
    xc = vs - jnp.mean(vs, axis=-1, keepdims=True)
    vsn = xc * lax.rsqrt(jnp.mean(xc * xc, axis=-1, keepdims=True) + EPS) * lg_ref[...] + lb_ref[...]
    if decode:
        outs[1][...] = vsn
    else:
        rows = vsn.shape[0]
        causal = (lax.broadcasted_iota(jnp.int32, (rows, rows), 1) <= lax.broadcasted_iota(jnp.int32, (rows, rows), 0))
    for g in range(N_HEADS):
        sl = slice(g * DV, (g + 1) * DV)
        o = o_ref[:, sl]
        o_a = o * lax.rsqrt(jnp.mean(o * o, axis=-1, keepdims=True) + EPS) * gn_ref[:, sl] * _silu(r_ref[:, sl])
        if decode:
            mixed = sgw_ref[:, sl] * vsn[:, sl] + sgb_ref[:, sl]
        else:
            w = jnp.where(causal, sgw_ref[g], 0.0).astype(BF16)
            mixed = _dot(w, vsn[:, sl].astype(BF16)) + sgb_ref[:, g:g + 1]
        o_b = _gelu_tanh(u_ref[:, sl]) * mixed
        m_ref[:, sl] = (jax.nn.sigmoid(ga_ref[:, sl]) * o_a + jax.nn.sigmoid(gb_ref[:, sl]) * o_b).astype(m_ref.dtype)


def _mixer(o, z_a, z_b, gn_row, lg_row, lb_row, sgw, sgb, decode):
    m = o.shape[0]
    tm = SG_CHUNK
    row_spec = pl.BlockSpec((1, D), lambda i: (0, 0))
    in_specs = [pl.BlockSpec((tm, D), lambda i: (i, 0)),
                pl.BlockSpec((tm, D), lambda i: (i, 2)),
                pl.BlockSpec((tm, D), lambda i: (i, 0)),
                pl.BlockSpec((tm, D), lambda i: (i, 1)),
                pl.BlockSpec((tm, D), lambda i: (i, 2)),
                pl.BlockSpec((tm, D), lambda i: (i, 3)),
                row_spec, row_spec, row_spec,
                pl.BlockSpec(sgw.shape, lambda i: (0,) * sgw.ndim),
                pl.BlockSpec(sgb.shape, lambda i: (0,) * sgb.ndim)]
    out_specs = [pl.BlockSpec((tm, D), lambda i: (i, 0))]
    out_shape = [jax.ShapeDtypeStruct((m, D), BF16)]
    if decode:
        out_specs.append(pl.BlockSpec((tm, D), lambda i: (i, 0)))
        out_shape.append(jax.ShapeDtypeStruct((m, D), F32))
    return pl.pallas_call(
        functools.partial(_mixer_kernel, decode=decode),
        grid=(m // tm,),
        in_specs=in_specs,
        out_specs=out_specs,
        out_shape=out_shape,
        compiler_params=_params(("arbitrary",), 2 * 8 * tm * D * 4),
        name="mixer_decode" if decode else "mixer_prefill",
    )(o, z_a, z_b, z_b, z_b, z_b, gn_row, lg_row, lb_row, sgw, sgb)


def _conv_gate(cur, p1, p2, cw_ref, cb_ref):
    return cb_ref[...] + cw_ref[0:1, :] * p2 + cw_ref[1:2, :] * p1 + cw_ref[2:3, :] * cur


def _conv_prefill_kernel(v_ref, g_ref, hv_ref, hg_ref, cwv_ref, cwg_ref, cbv_ref, cbg_ref, o_ref, *, tiles_per_seq):
    first = (pl.program_id(0) % tiles_per_seq) == 0
    row = lax.broadcasted_iota(jnp.int32, v_ref.shape, 0)

    def conv(cur_ref, halo_ref, cw_ref, cb_ref):
        cur = cur_ref[...]
        halo = jnp.where(first, 0.0, halo_ref[...])
        p1 = jnp.where(row == 0, halo[7:8], pltpu.roll(cur, 1, 0))
        p2 = jnp.where(row == 0, halo[6:7], jnp.where(row == 1, halo[7:8], pltpu.roll(cur, 2, 0)))
        return _conv_gate(cur, p1, p2, cw_ref, cb_ref)

    val = conv(v_ref, hv_ref, cwv_ref, cbv_ref)
    gate = conv(g_ref, hg_ref, cwg_ref, cbg_ref)
    o_ref[...] = (_silu(gate) * val).astype(o_ref.dtype)


def _conv_prefill(up, conv_w, conv_b, layer, seq):
    m = up.shape[0]
    tc, tn = 128, F // 2
    nj = F // tn
    halo = lambda i: jnp.maximum(i * (tc // 8) - 1, 0)
    return pl.pallas_call(
        functools.partial(_conv_prefill_kernel, tiles_per_seq=seq // tc),
        grid=(m // tc, nj),
        in_specs=[pl.BlockSpec((tc, tn), lambda i, j: (i, j)),
                  pl.BlockSpec((tc, tn), lambda i, j: (i, nj + j)),
                  pl.BlockSpec((8, tn), lambda i, j: (halo(i), j)),
                  pl.BlockSpec((8, tn), lambda i, j: (halo(i), nj + j)),
                  pl.BlockSpec((None, 3, tn), lambda i, j: (layer, 0, j)),
                  pl.BlockSpec((None, 3, tn), lambda i, j: (layer, 0, nj + j)),
                  pl.BlockSpec((None, 1, tn), lambda i, j: (layer, 0, j)),
                  pl.BlockSpec((None, 1, tn), lambda i, j: (layer, 0, nj + j))],
        out_specs=pl.BlockSpec((tc, tn), lambda i, j: (i, j)),
        out_shape=jax.ShapeDtypeStruct((m, F), BF16),
        compiler_params=_params(("arbitrary", "arbitrary"), 2 * 3 * tc * tn * 4),
        name="conv_prefill",
    )(up, up, up, up, conv_w, conv_w, conv_b, conv_b)


def _conv_decode_kernel(v_ref, g_ref, v1_ref, g1_ref, v2_ref, g2_ref, cwv_ref, cwg_ref, cbv_ref, cbg_ref, o_ref):
    val = _conv_gate(v_ref[...], v1_ref[...], v2_ref[...], cwv_ref, cbv_ref)
    gate = _conv_gate(g_ref[...], g1_ref[...], g2_ref[...], cwg_ref, cbg_ref)
    o_ref[...] = (_silu(gate) * val).astype(o_ref.dtype)


def _conv_decode(up, conv_state, conv_w, conv_b, layer):
    m = up.shape[0]
    tn = F // 2
    nj = F // tn
    cur = lambda off: pl.BlockSpec((m, tn), lambda j: (0, off + j))
    old = lambda off: pl.BlockSpec((None, m, tn), lambda j: (layer, 0, off + j))
    par = lambda rows, off: pl.BlockSpec((None, rows, tn), lambda j: (layer, 0, off + j))
    return pl.pallas_call(
        _conv_decode_kernel,
        grid=(nj,),
        in_specs=[cur(0), cur(nj), old(2 * nj), old(3 * nj), old(0), old(nj),
                  par(3, 0), par(3, nj), par(1, 0), par(1, nj)],
        out_specs=pl.BlockSpec((m, tn), lambda j: (0, j)),
        out_shape=jax.ShapeDtypeStruct((m, F), BF16),
        compiler_params=_params(("arbitrary",), 2 * 7 * m * tn * 4),
        name="conv_decode",
    )(up, up, conv_state, conv_state, conv_state, conv_state, conv_w, conv_w, conv_b, conv_b)


class _Weights(NamedTuple):
    norm1_g: jax.Array
    w_in: jax.Array
    w_tail: jax.Array
    wg: jax.Array
    bg: jax.Array
    gla_norm_g: jax.Array
    sg_ln_g: jax.Array
    sg_ln_b: jax.Array
    w_s: jax.Array
    b_s: jax.Array
    w_o: jax.Array
    norm2_g: jax.Array
    w_up: jax.Array
    conv_w: jax.Array
    conv_b: jax.Array
    w_down: jax.Array
    final_g: jax.Array


def _trunk(x, mod, wts, tm, seq, gla_state, conv_state):
    m = x.shape[0]
    decode = seq == 1
    tr = min(tm, 256)
    gla_buf = None
    conv_rows, v_rows = [], []
    for l in range(DEPTH):
        row = lambda a: a[l].reshape(1, -1)
        h = _norm(x, row(wts.norm1_g), tr, BF16, mod, l, comp_shift=0, comp_scale=1)
        z_a = _matmul(h, wts.w_in, l, 0, QKVR_COLS, tm, 512, name="in_proj")
        alr = _matmul(h, wts.w_in, l, ALR_COL // LANES, LANES, tm, LANES, name="gate_lowrank")
        z_b = _matmul(h, wts.w_tail, l, 0, 4 * D, tm, 1024, name="in_proj_tail")
        if decode:
            o, gla_buf = _gla_decode(z_a, alr, wts.wg, wts.bg, l, gla_state, gla_buf)
            sgw = jnp.repeat(wts.w_s[l, :, 0, 0], DV).reshape(1, D)
            sgb = jnp.repeat(wts.b_s[l, :, 0], DV).reshape(1, D)
        else:
            o, gla_buf = _gla_prefill(z_a, alr, wts.wg, wts.bg, l, gla_buf, m // seq, seq)
            sgw = wts.w_s[l]
            sgb = wts.b_s[l].T
        mixed = _mixer(o, z_a, z_b, row(wts.gla_norm_g), row(wts.sg_ln_g), row(wts.sg_ln_b), sgw, sgb, decode)
        if decode:
            v_rows.append(mixed[1])
        x = _matmul(mixed[0], wts.w_o, l, 0, D, tm, 512, res=x, mod=mod, gate_comp=2, name="out_proj")
        h2 = _norm(x, row(wts.norm2_g), tr, BF16, mod, l, comp_shift=3, comp_scale=4)
        up = _matmul(h2, wts.w_up, l, 0, 2 * F, tm, 512, name="up_proj")
        if decode:
            act = _conv_decode(up, conv_state, wts.conv_w, wts.conv_b, l)
            conv_rows.append(jnp.stack([conv_state[l, :, 2 * F:], up], axis=1))
        else:
            act = _conv_prefill(up, wts.conv_w, wts.conv_b, l, seq)
            conv_rows.append(up.reshape(m // seq, seq, 2 * F)[:, seq - 2:])
        x = _matmul(act, wts.w_down, l, 0, D, tm, 512, nk=2, res=x, mod=mod, gate_comp=5, name="down_proj")
    y = _norm(x, wts.final_g.reshape(1, D), tr, F32)
    return y, gla_buf, jnp.stack(conv_rows), v_rows


def kernel(x_prompt, x_sample, state_gla, state_conv, c_prompt, c_sample, w_ada, b_ada, norm1_g, w_in, w_gate2,
           b_gate2, gla_norm_g, sg_ln_g, sg_ln_b, w_s, b_s, w_o, norm2_g, w_up, conv_w, conv_b, w_down, final_g):
    batch, seq, _ = x_prompt.shape
    dec_batch = x_sample.shape[0]
    assert x_sample.shape[1] == 1 and seq % SG_CHUNK == 0

    pad = (-(dec_batch + batch)) % 8
    c_all = jnp.concatenate([c_sample, c_prompt, jnp.zeros((pad, D), F32)], axis=0)
    mod = _ada(c_all, w_ada, b_ada)
    mod_s = _Mod(mod[:, :dec_batch].reshape(DEPTH, 1, dec_batch, N_MOD * D), dec_batch)
    mod_p = _Mod(mod[:, dec_batch:dec_batch + batch].reshape(DEPTH, batch, 1, N_MOD * D), seq)

    wts = _Weights(
        norm1_g=norm1_g, w_in=w_in, w_tail=w_in[:, :, TAIL_COL:].astype(BF16),
        wg=jnp.pad(w_gate2, ((0, 0), (0, LANES - RANK), (0, 0))), bg=b_gate2.reshape(DEPTH, 1, N_HEADS * DK),
        gla_norm_g=gla_norm_g, sg_ln_g=sg_ln_g, sg_ln_b=sg_ln_b, w_s=w_s, b_s=b_s, w_o=w_o, norm2_g=norm2_g,
        w_up=w_up, conv_w=conv_w, conv_b=conv_b.reshape(DEPTH, 1, 2 * F), w_down=w_down.astype(BF16), final_g=final_g)

    y_p, gla_p, conv_p, _ = _trunk(x_prompt.reshape(batch * seq, D), mod_p, wts, 1024, seq, None, None)
    y_s, gla_s, conv_s, v_rows = _trunk(x_sample.reshape(dec_batch, D), mod_s, wts, dec_batch, 1, state_gla,
                                        state_conv.reshape(DEPTH, dec_batch, 4 * F))
    return (y_p.reshape(batch, seq, D), y_s.reshape(dec_batch, 1, D), gla_p, conv_p, gla_s, conv_s,
            jnp.stack(v_rows).reshape(DEPTH, dec_batch, 1, D))
```

```python
import functools
from typing import NamedTuple

import jax
import jax.numpy as jnp
from jax import lax
from jax.experimental import pallas as pl
from jax.experimental.pallas import tpu as pltpu

F32 = jnp.float32
BF16 = jnp.bfloat16

D = 4096
DEPTH = 4
N_HEADS = 4
DK = 512
DV = 1024
RANK = 16
TAU = 16.0
SG_CHUNK = 128
F = 11008
N_MOD = 6
EPS = 1e-6

QKVR_COLS = 2 * N_HEADS * DK + 2 * D
ALR_COL = QKVR_COLS
TAIL_COL = QKVR_COLS + RANK

LANES = 128
VMEM_CAP = 56 * 2 ** 20
EXP_CLAMP = 80.0


def _vmem(nbytes):
    return int(min(VMEM_CAP, nbytes + 16 * 2 ** 20))


def _params(semantics, nbytes):
    return pltpu.CompilerParams(dimension_semantics=semantics, vmem_limit_bytes=_vmem(nbytes))


def _silu(x):
    return x * jax.nn.sigmoid(x)


def _gelu_tanh(x):
    return 0.5 * x * (1.0 + jnp.tanh(0.7978845608028654 * (x + 0.044715 * x * x * x)))


def _log_sigmoid(x):
    return jnp.minimum(x, 0.0) - jnp.log(1.0 + jnp.exp(-jnp.abs(x)))


def _split_bf16(x):
    hi = x.astype(BF16)
    lo = (x - hi.astype(F32)).astype(BF16)
    return hi, lo


def _dot(a, b):
    return jnp.dot(a, b, preferred_element_type=F32)


def _dot_f32(a, b):
    a_hi, a_lo = _split_bf16(a)
    b_hi, b_lo = _split_bf16(b)
    return _dot(a_hi, b_hi) + _dot(a_hi, b_lo) + _dot(a_lo, b_hi)


class _Mod(NamedTuple):
    arr: jax.Array
    rows_per_group: int


def _mod_spec(mod, layer, comp, tm, tn):
    rows = mod.arr.shape[2]
    nblk = D // tn
    if rows == 1:
        return pl.BlockSpec((None, None, 1, tn),
                            lambda i, *r: (layer, (i * tm) // mod.rows_per_group, 0, comp * nblk + (r[0] if r else 0)))
    assert rows % tm == 0
    return pl.BlockSpec((None, None, tm, tn), lambda i, *r: (layer, 0, i, comp * nblk + (r[0] if r else 0)))


def _ada_kernel(c_ref, w_ref, b_ref, o_ref):
    c = c_ref[...]
    o_ref[...] = _dot(_silu(c).astype(BF16), w_ref[...].astype(BF16)) + b_ref[...]


def _ada(c_all, w_ada, b_ada):
    rows = c_all.shape[0]
    tn = 512
    n = N_MOD * D
    return pl.pallas_call(
        _ada_kernel,
        grid=(DEPTH, n // tn),
        in_specs=[pl.BlockSpec((rows, D), lambda l, j: (0, 0)),
                  pl.BlockSpec((None, D, tn), lambda l, j: (l, 0, j)),
                  pl.BlockSpec((None, 1, tn), lambda l, j: (l, 0, j))],
        out_specs=pl.BlockSpec((None, rows, tn), lambda l, j: (l, 0, j)),
        out_shape=jax.ShapeDtypeStruct((DEPTH, rows, n), F32),
        compiler_params=_params(("arbitrary", "arbitrary"), 2 * (rows * D * 4 + D * tn * 4) + D * tn * 2),
        name="ada_mod",
    )(c_all, w_ada, b_ada.reshape(DEPTH, 1, n))


def _norm_kernel(*refs, modulate):
    if modulate:
        x_ref, g_ref, sc_ref, sh_ref, o_ref = refs
    else:
        x_ref, g_ref, o_ref = refs
    x = x_ref[...]
    y = x * lax.rsqrt(jnp.mean(x * x, axis=-1, keepdims=True) + EPS) * g_ref[...]
    if modulate:
        y = y * (1.0 + sc_ref[...]) + sh_ref[...]
    o_ref[...] = y.astype(o_ref.dtype)


def _norm(x, g_row, tm, out_dtype, mod=None, layer=None, comp_shift=None, comp_scale=None):
    m = x.shape[0]
    in_specs = [pl.BlockSpec((tm, D), lambda i: (i, 0)), pl.BlockSpec((1, D), lambda i: (0, 0))]
    args = [x, g_row]
    if mod is not None:
        in_specs += [_mod_spec(mod, layer, comp_scale, tm, D), _mod_spec(mod, layer, comp_shift, tm, D)]
        args += [mod.arr, mod.arr]
    return pl.pallas_call(
        functools.partial(_norm_kernel, modulate=mod is not None),
        grid=(m // tm,),
        in_specs=in_specs,
        out_specs=pl.BlockSpec((tm, D), lambda i: (i, 0)),
        out_shape=jax.ShapeDtypeStruct((m, D), out_dtype),
        compiler_params=_params(("arbitrary",), 2 * tm * D * 4 * 4),
        name="norm_mod" if mod is not None else "norm",
    )(*args)


def _matmul_kernel(*refs, nk, residual, w_out_major):
    if residual:
        a_ref, w_ref, res_ref, gate_ref, o_ref = refs[:5]
    else:
        a_ref, w_ref, o_ref = refs[:3]
    acc_ref = refs[-1] if nk > 1 else None
    if w_out_major:
        part = lax.dot_general(a_ref[...], w_ref[...].astype(BF16), (((1,), (1,)), ((), ())),
                               preferred_element_type=F32)
    else:
        part = _dot(a_ref[...], w_ref[...].astype(BF16))

    def finish(acc):
        if residual:
            acc = res_ref[...] + gate_ref[...] * acc
        o_ref[...] = acc.astype(o_ref.dtype)

    if nk == 1:
        finish(part)
    else:
        kk = pl.program_id(2)

        @pl.when(kk == 0)
        def _():
            acc_ref[...] = part

        @pl.when(kk > 0)
        def _():
            acc_ref[...] += part

        @pl.when(kk == nk - 1)
        def _():
            finish(acc_ref[...])


def _matmul(a, w, layer, col0, n, tm, tn, nk=1, out_dtype=F32, res=None, mod=None, gate_comp=None, w_out_major=False,
            name="proj"):
    m, k = a.shape
    tk = k // nk
    assert m % tm == 0 and n % tn == 0 and k % nk == 0
    residual = res is not None
    if w_out_major:
        assert col0 % 8 == 0
        w_spec = pl.BlockSpec((None, pl.Element(tn), pl.Element(tk)), lambda i, j, kk: (layer, pl.multiple_of(col0 + j * tn, 8), pl.multiple_of(kk * tk, LANES)))
    else:
        assert col0 % tn == 0
        w_spec = pl.BlockSpec((None, tk, tn), lambda i, j, kk: (layer, kk, col0 // tn + j))
    in_specs = [pl.BlockSpec((tm, tk), lambda i, j, kk: (i, kk)), w_spec]
    args = [a, w]
    nbytes = 2 * (tm * tk * 2 + tk * tn * w.dtype.itemsize + tm * tn * 4) + tk * tn * 2 + 2 * tm * tn * 4
    if residual:
        in_specs += [pl.BlockSpec((tm, tn), lambda i, j, kk: (i, j)), _mod_spec(mod, layer, gate_comp, tm, tn)]
        args += [res, mod.arr]
        nbytes += 2 * 2 * tm * tn * 4
    return pl.pallas_call(
        functools.partial(_matmul_kernel, nk=nk, residual=residual, w_out_major=w_out_major),
        grid=(m // tm, n // tn, nk),
        in_specs=in_specs,
        out_specs=pl.BlockSpec((tm, tn), lambda i, j, kk: (i, j)),
        out_shape=jax.ShapeDtypeStruct((m, n), out_dtype),
        scratch_shapes=[pltpu.VMEM((tm, tn), F32)] if nk > 1 else [],
        compiler_params=_params(("arbitrary", "arbitrary", "arbitrary"), nbytes),
        name=name,
    )(*args)


def _gate_log_decay(alr, wg_ref, bg_ref):
    return _log_sigmoid(_dot_f32(alr, wg_ref[...]) + bg_ref[...]) * (1.0 / TAU)


def _gla_prefill_kernel(q_ref, k_ref, v_ref, alr_ref, wg_ref, bg_ref, *rest, chunk, sub):
    o_ref, s_ref = rest[-2:]

    @pl.when(pl.program_id(2) == 0)
    def _():
        s_ref[...] = jnp.zeros_like(s_ref)

    q = q_ref[...] * (DK ** -0.5)
    k = k_ref[...]
    v = v_ref[...].astype(BF16)
    la = _gate_log_decay(alr_ref[...], wg_ref, bg_ref)
    row = lax.broadcasted_iota(jnp.int32, (chunk, chunk), 0)
    col = lax.broadcasted_iota(jnp.int32, (chunk, chunk), 1)
    causal = col <= row
    tri = jnp.where(causal, 1.0, 0.0).astype(BF16)
    la_hi, la_lo = _split_bf16(la)
    b = _dot(tri, la_hi) + _dot(tri, la_lo)
    s_old = s_ref[...]
    o = _dot((q * jnp.exp(b)).astype(BF16), s_old.astype(BF16))
    blocks = []
    for r0 in range(0, chunk, sub):
        mid = b[r0 + sub // 2 - 1:r0 + sub // 2, :]
        qt = q[r0:r0 + sub] * jnp.exp(jnp.clip(b[r0:r0 + sub] - mid, -EXP_CLAMP, EXP_CLAMP))
        kt = k * jnp.exp(jnp.clip(mid - b, -EXP_CLAMP, EXP_CLAMP))
        blocks.append(lax.dot_general(qt.astype(BF16), kt.astype(BF16), (((1,), (1,)), ((), ())),
                                      preferred_element_type=F32))
    scores = jnp.where(causal, jnp.concatenate(blocks, axis=0), 0.0)
    o_ref[...] = o + _dot(scores.astype(BF16), v)
    k_t = k.T
    b_t = b.T
    b_last = b_t[:, chunk - 1:chunk]
    s_ref[...] = jnp.exp(b_last) * s_old + _dot((k_t * jnp.exp(b_last - b_t)).astype(BF16), v)


def _gla_prefill(z_a, alr, wg, bg, layer, state_buf, batch, seq):
    chunk, sub = 128, 32
    nc = seq // chunk
    in_specs = [pl.BlockSpec((chunk, DK), lambda b, h, n: (b * nc + n, h)),
                pl.BlockSpec((chunk, DK), lambda b, h, n: (b * nc + n, N_HEADS + h)),
                pl.BlockSpec((chunk, DV), lambda b, h, n: (b * nc + n, N_HEADS + h)),
                pl.BlockSpec((chunk, LANES), lambda b, h, n: (b * nc + n, 0)),
                pl.BlockSpec((None, LANES, DK), lambda b, h, n: (layer, 0, h)),
                pl.BlockSpec((None, 1, DK), lambda b, h, n: (layer, 0, h))]
    args = [z_a, z_a, z_a, alr, wg, bg]
    aliases = {}
    if state_buf is not None:
        in_specs.append(pl.BlockSpec(memory_space=pl.ANY))
        args.append(state_buf)
        aliases = {len(args) - 1: 1}
    return pl.pallas_call(
        functools.partial(_gla_prefill_kernel, chunk=chunk, sub=sub),
        grid=(batch, N_HEADS, nc),
        in_specs=in_specs,
        out_specs=[pl.BlockSpec((chunk, DV), lambda b, h, n: (b * nc + n, h)),
                   pl.BlockSpec((None, None, None, DK, DV), lambda b, h, n: (layer, b, h, 0, 0))],
        out_shape=[jax.ShapeDtypeStruct((batch * seq, N_HEADS * DV), F32),
                   jax.ShapeDtypeStruct((DEPTH, batch, N_HEADS, DK, DV), F32)],
        input_output_aliases=aliases,
        compiler_params=_params(("arbitrary", "arbitrary", "arbitrary"), 4 * DK * DV * 4),
        name="gla_prefill",
    )(*args)


def _gla_decode_kernel(q_ref, k_ref, v_ref, alr_ref, wg_ref, bg_ref, s_ref, *rest, heads):
    o_ref, so_ref = rest[-2:]
    q = q_ref[...] * (DK ** -0.5)
    la = _gate_log_decay(jnp.broadcast_to(alr_ref[...], (8, LANES)), wg_ref, bg_ref)[0:1]
    rid = lax.broadcasted_iota(jnp.int32, (LANES, heads * DK), 0)
    rows = jnp.where(rid == 0, q, jnp.where(rid == 1, k_ref[...], jnp.where(rid == 2, la, 0.0)))
    cols = rows.T
    for h in range(heads):
        c = cols[h * DK:(h + 1) * DK]
        s_new = jnp.exp(c[:, 2:3]) * s_ref[h] + c[:, 1:2] * v_ref[:, h * DV:(h + 1) * DV]
        so_ref[h] = s_new
        o_ref[:, h * DV:(h + 1) * DV] = jnp.sum(c[:, 0:1] * s_new, axis=0, keepdims=True)


def _gla_decode(z_a, alr, wg, bg, layer, state_in, state_buf):
    batch = z_a.shape[0]
    heads = 2
    z3 = z_a.reshape(batch, 1, QKVR_COLS)
    k_blk = N_HEADS // heads
    in_specs = [pl.BlockSpec((None, 1, heads * DK), lambda b, h: (b, 0, h)),
                pl.BlockSpec((None, 1, heads * DK), lambda b, h: (b, 0, k_blk + h)),
                pl.BlockSpec((None, 1, heads * DV), lambda b, h: (b, 0, k_blk + h)),
                pl.BlockSpec((None, 1, LANES), lambda b, h: (b, 0, 0)),
                pl.BlockSpec((None, LANES, heads * DK), lambda b, h: (layer, 0, h)),
                pl.BlockSpec((None, 1, heads * DK), lambda b, h: (layer, 0, h)),
                pl.BlockSpec((None, None, heads, DK, DV), lambda b, h: (layer, b, h, 0, 0))]
    args = [z3, z3, z3, alr.reshape(batch, 1, LANES), wg, bg, state_in]
    aliases = {}
    if state_buf is not None:
        in_specs.append(pl.BlockSpec(memory_space=pl.ANY))
        args.append(state_buf)
        aliases = {len(args) - 1: 1}
    o, state = pl.pallas_call(
        functools.partial(_gla_decode_kernel, heads=heads),
        grid=(batch, N_HEADS // heads),
        in_specs=in_specs,
        out_specs=[pl.BlockSpec((None, 1, heads * DV), lambda b, h: (b, 0, h)),
                   pl.BlockSpec((None, None, heads, DK, DV), lambda b, h: (layer, b, h, 0, 0))],
        out_shape=[jax.ShapeDtypeStruct((batch, 1, N_HEADS * DV), F32),
                   jax.ShapeDtypeStruct((DEPTH, batch, N_HEADS, DK, DV), F32)],
        input_output_aliases=aliases,
        compiler_params=_params(("arbitrary", "arbitrary"), (4 * heads + 2) * DK * DV * 4),
        name="gla_decode",
    )(*args)
    return o.reshape(batch, N_HEADS * DV), state


def _mixer_kernel(o_ref, r_ref, u_ref, vs_ref, ga_ref, gb_ref, gn_ref, lg_ref, lb_ref, sgw_ref, sgb_ref,
                  *outs, decode):
    m_ref = outs[0]
    vs = _gelu_tanh(vs_ref[...])
    xc = vs - jnp.mean(vs, axis=-1, keepdims=True)
    vsn = xc * lax.rsqrt(jnp.mean(xc * xc, axis=-1, keepdims=True) + EPS) * lg_ref[...] + lb_ref[...]
    if decode:
        outs[1][...] = vsn
    else:
        rows = vsn.shape[0]
        causal = (lax.broadcasted_iota(jnp.int32, (rows, rows), 1) <= lax.broadcasted_iota(jnp.int32, (rows, rows), 0))
    for g in range(N_HEADS):
        sl = slice(g * DV, (g + 1) * DV)
        o = o_ref[:, sl]
        o_a = o * lax.rsqrt(jnp.mean(o * o, axis=-1, keepdims=True) + EPS) * gn_ref[:, sl] * _silu(r_ref[:, sl])
        if decode:
            mixed = sgw_ref[:, sl] * vsn[:, sl] + sgb_ref[:, sl]
        else:
            w = jnp.where(causal, sgw_ref[g], 0.0).astype(BF16)
            mixed = _dot(w, vsn[:, sl].astype(BF16)) + sgb_ref[:, g:g + 1]
        o_b = _gelu_tanh(u_ref[:, sl]) * mixed
        m_ref[:, sl] = (jax.nn.sigmoid(ga_ref[:, sl]) * o_a + jax.nn.sigmoid(gb_ref[:, sl]) * o_b).astype(m_ref.dtype)


def _mixer(o, z_a, z_b, gn_row, lg_row, lb_row, sgw, sgb, decode):
    m = o.shape[0]
    tm = SG_CHUNK
    row_spec = pl.BlockSpec((1, D), lambda i: (0, 0))
    in_specs = [pl.BlockSpec((tm, D), lambda i: (i, 0)),
                pl.BlockSpec((tm, D), lambda i: (i, 2)),
                pl.BlockSpec((tm, D), lambda i: (i, 0)),
                pl.BlockSpec((tm, D), lambda i: (i, 1)),
                pl.BlockSpec((tm, D), lambda i: (i, 2)),
                pl.BlockSpec((tm, D), lambda i: (i, 3)),
                row_spec, row_spec, row_spec,
                pl.BlockSpec(sgw.shape, lambda i: (0,) * sgw.ndim),
                pl.BlockSpec(sgb.shape, lambda i: (0,) * sgb.ndim)]
    out_specs = [pl.BlockSpec((tm, D), lambda i: (i, 0))]
    out_shape = [jax.ShapeDtypeStruct((m, D), BF16)]
    if decode:
        out_specs.append(pl.BlockSpec((tm, D), lambda i: (i, 0)))
        out_shape.append(jax.ShapeDtypeStruct((m, D), F32))
    return pl.pallas_call(
        functools.partial(_mixer_kernel, decode=decode),
        grid=(m // tm,),
        in_specs=in_specs,
        out_specs=out_specs,
        out_shape=out_shape,
        compiler_params=_params(("arbitrary",), 2 * 8 * tm * D * 4),
        name="mixer_decode" if decode else "mixer_prefill",
    )(o, z_a, z_b, z_b, z_b, z_b, gn_row, lg_row, lb_row, sgw, sgb)


def _conv_gate(cur, p1, p2, cw_ref, cb_ref):
    return cb_ref[...] + cw_ref[0:1, :] * p2 + cw_ref[1:2, :] * p1 + cw_ref[2:3, :] * cur


def _up_conv_kernel(a_ref, ah_ref, wv_ref, wg_ref, cwv_ref, cwg_ref, cbv_ref, cbg_ref, act_ref, tv_ref, tg_ref, *,
                    tiles_per_seq):
    first = (pl.program_id(0) % tiles_per_seq) == 0
    a = a_ref[...]
    ah = ah_ref[...]
    tm = a.shape[0]
    row = lax.broadcasted_iota(jnp.int32, act_ref.shape, 0)

    def branch(w_ref, cw_ref, cb_ref, tail_ref):
        w = w_ref[...].astype(BF16)
        cur = _dot(a, w)
        halo = jnp.where(first, 0.0, _dot(ah, w))
        tail_ref[...] = cur[tm - 8:tm]
        p1 = jnp.where(row == 0, halo[15:16], pltpu.roll(cur, 1, 0))
        p2 = jnp.where(row == 0, halo[14:15], jnp.where(row == 1, halo[15:16], pltpu.roll(cur, 2, 0)))
        return _conv_gate(cur, p1, p2, cw_ref, cb_ref)

    val = branch(wv_ref, cwv_ref, cbv_ref, tv_ref)
    gate = branch(wg_ref, cwg_ref, cbg_ref, tg_ref)
    act_ref[...] = (_silu(gate) * val).astype(act_ref.dtype)


def _up_conv_prefill(h, w_up, conv_w, conv_b, layer, seq, tm):
    m = h.shape[0]
    tn = 256
    nj = F // tn
    halo = lambda i: jnp.maximum(i * (tm // 16) - 1, 0)
    tiles_per_seq = seq // tm
    wcol = lambda rows, off: pl.BlockSpec((None, rows, tn), lambda i, j: (layer, 0, off + j))
    tail = pl.BlockSpec((8, tn), lambda i, j: (i, j))
    act, tail_v, tail_g = pl.pallas_call(
        functools.partial(_up_conv_kernel, tiles_per_seq=tiles_per_seq),
        grid=(m // tm, nj),
        in_specs=[pl.BlockSpec((tm, D), lambda i, j: (i, 0)),
                  pl.BlockSpec((16, D), lambda i, j: (halo(i), 0)),
                  wcol(D, 0), wcol(D, nj), wcol(3, 0), wcol(3, nj), wcol(1, 0), wcol(1, nj)],
        out_specs=[pl.BlockSpec((tm, tn), lambda i, j: (i, j)), tail, tail],
        out_shape=[jax.ShapeDtypeStruct((m, F), BF16),
                   jax.ShapeDtypeStruct((m // tm * 8, F), F32), jax.ShapeDtypeStruct((m // tm * 8, F), F32)],
        compiler_params=_params(("arbitrary", "arbitrary"),
                                2 * (tm * D * 2 + 2 * D * tn * 4) + 2 * D * tn * 2 + 8 * tm * tn * 4),
        name="up_conv_prefill",
    )(h, h, w_up, w_up, conv_w, conv_w, conv_b, conv_b)
    tails = jnp.concatenate([tail_v, tail_g], axis=-1).reshape(m // tm, 8, 2 * F)
    return act, tails[tiles_per_seq - 1::tiles_per_seq, 6:8]


def _conv_decode_kernel(v_ref, g_ref, v1_ref, g1_ref, v2_ref, g2_ref, cwv_ref, cwg_ref, cbv_ref, cbg_ref, o_ref):
    val = _conv_gate(v_ref[...], v1_ref[...], v2_ref[...], cwv_ref, cbv_ref)
    gate = _conv_gate(g_ref[...], g1_ref[...], g2_ref[...], cwg_ref, cbg_ref)
    o_ref[...] = (_silu(gate) * val).astype(o_ref.dtype)


def _conv_decode(up, conv_state, conv_w, conv_b, layer):
    m = up.shape[0]
    tn = F // 2
    nj = F // tn
    cur = lambda off: pl.BlockSpec((m, tn), lambda j: (0, off + j))
    old = lambda off: pl.BlockSpec((None, m, tn), lambda j: (layer, 0, off + j))
    par = lambda rows, off: pl.BlockSpec((None, rows, tn), lambda j: (layer, 0, off + j))
    return pl.pallas_call(
        _conv_decode_kernel,
        grid=(nj,),
        in_specs=[cur(0), cur(nj), old(2 * nj), old(3 * nj), old(0), old(nj),
                  par(3, 0), par(3, nj), par(1, 0), par(1, nj)],
        out_specs=pl.BlockSpec((m, tn), lambda j: (0, j)),
        out_shape=jax.ShapeDtypeStruct((m, F), BF16),
        compiler_params=_params(("arbitrary",), 2 * 7 * m * tn * 4),
        name="conv_decode",
    )(up, up, conv_state, conv_state, conv_state, conv_state, conv_w, conv_w, conv_b, conv_b)


class _Weights(NamedTuple):
    norm1_g: jax.Array
    w_in_t: jax.Array
    wg: jax.Array
    bg: jax.Array
    gla_norm_g: jax.Array
    sg_ln_g: jax.Array
    sg_ln_b: jax.Array
    w_s: jax.Array
    b_s: jax.Array
    w_o: jax.Array
    norm2_g: jax.Array
    w_up: jax.Array
    conv_w: jax.Array
    conv_b: jax.Array
    w_down: jax.Array
    final_g: jax.Array


def _trunk(x, mod, wts, tm, seq, gla_state, conv_state):
    m = x.shape[0]
    decode = seq == 1
    tr = min(tm, 256)
    gla_buf = None
    conv_rows, v_rows = [], []
    for l in range(DEPTH):
        row = lambda a: a[l].reshape(1, -1)
        h = _norm(x, row(wts.norm1_g), tr, BF16, mod, l, comp_shift=0, comp_scale=1)
        z_a = _matmul(h, wts.w_in_t, l, 0, QKVR_COLS, tm, 512, w_out_major=True, name="in_proj")
        alr = _matmul(h, wts.w_in_t, l, ALR_COL, LANES, tm, LANES, w_out_major=True, name="gate_lowrank")
        z_b = _matmul(h, wts.w_in_t, l, TAIL_COL, 4 * D, tm, 512, w_out_major=True, name="in_proj_tail")
        if decode:
            o, gla_buf = _gla_decode(z_a, alr, wts.wg, wts.bg, l, gla_state, gla_buf)
            sgw = jnp.repeat(wts.w_s[l, :, 0, 0], DV).reshape(1, D)
            sgb = jnp.repeat(wts.b_s[l, :, 0], DV).reshape(1, D)
        else:
            o, gla_buf = _gla_prefill(z_a, alr, wts.wg, wts.bg, l, gla_buf, m // seq, seq)
            sgw = wts.w_s[l]
            sgb = wts.b_s[l].T
        mixed = _mixer(o, z_a, z_b, row(wts.gla_norm_g), row(wts.sg_ln_g), row(wts.sg_ln_b), sgw, sgb, decode)
        if decode:
            v_rows.append(mixed[1])
        x = _matmul(mixed[0], wts.w_o, l, 0, D, tm, 512, res=x, mod=mod, gate_comp=2, name="out_proj")
        h2 = _norm(x, row(wts.norm2_g), tr, BF16, mod, l, comp_shift=3, comp_scale=4)
        if decode:
            up = _matmul(h2, wts.w_up, l, 0, 2 * F, tm, 512, name="up_proj")
            act = _conv_decode(up, conv_state, wts.conv_w, wts.conv_b, l)
            conv_rows.append(jnp.stack([conv_state[l, :, 2 * F:], up], axis=1))
        else:
            act, new_conv = _up_conv_prefill(h2, wts.w_up, wts.conv_w, wts.conv_b, l, seq, tm)
            conv_rows.append(new_conv)
        x = _matmul(act, wts.w_down, l, 0, D, tm, 512, nk=2, res=x, mod=mod, gate_comp=5, name="down_proj")
    y = _norm(x, wts.final_g.reshape(1, D), tr, F32)
    return y, gla_buf, jnp.stack(conv_rows), v_rows


def kernel(x_prompt, x_sample, state_gla, state_conv, c_prompt, c_sample, w_ada, b_ada, norm1_g, w_in, w_gate2,
           b_gate2, gla_norm_g, sg_ln_g, sg_ln_b, w_s, b_s, w_o, norm2_g, w_up, conv_w, conv_b, w_down, final_g):
    batch, seq, _ = x_prompt.shape
    dec_batch = x_sample.shape[0]
    assert x_sample.shape[1] == 1 and seq % SG_CHUNK == 0

    pad = (-(dec_batch + batch)) % 8
    c_all = jnp.concatenate([c_sample, c_prompt, jnp.zeros((pad, D), F32)], axis=0)
    mod = _ada(c_all, w_ada, b_ada)
    mod_s = _Mod(mod[:, :dec_batch].reshape(DEPTH, 1, dec_batch, N_MOD * D), dec_batch)
    mod_p = _Mod(mod[:, dec_batch:dec_batch + batch].reshape(DEPTH, batch, 1, N_MOD * D), seq)

    wts = _Weights(
        norm1_g=norm1_g, w_in_t=jnp.swapaxes(w_in, 1, 2),
        wg=jnp.pad(w_gate2, ((0, 0), (0, LANES - RANK), (0, 0))), bg=b_gate2.reshape(DEPTH, 1, N_HEADS * DK),
        gla_norm_g=gla_norm_g, sg_ln_g=sg_ln_g, sg_ln_b=sg_ln_b, w_s=w_s, b_s=b_s, w_o=w_o, norm2_g=norm2_g,
        w_up=w_up, conv_w=conv_w, conv_b=conv_b.reshape(DEPTH, 1, 2 * F), w_down=w_down.astype(BF16), final_g=final_g)

    y_p, gla_p, conv_p, _ = _trunk(x_prompt.reshape(batch * seq, D), mod_p, wts, 1024, seq, None, None)
    y_s, gla_s, conv_s, v_rows = _trunk(x_sample.reshape(dec_batch, D), mod_s, wts, dec_batch, 1, state_gla,
                                        state_conv.reshape(DEPTH, dec_batch, 4 * F))
    return (y_p.reshape(batch, seq, D), y_s.reshape(dec_batch, 1, D), gla_p, conv_p, gla_s, conv_s,
            jnp.stack(v_rows).reshape(DEPTH, dec_batch, 1, D))
```
